```python
import jax, jax.numpy as jnp
from jax import lax
import numpy as np

D_MODEL = 1024
BATCH = 8
SEQ = 2048
DEPTH = 2
DEC_BATCH = 32
DEC_SEQ = 1
PAST_LEN = 8192
PAGE_SIZE = 128

N_MIXERS = 2
FOX_HEADS = 16
FOX_HEAD_DIM = D_MODEL // FOX_HEADS
Q_BLOCK = 128
RWKV_HEAD_DIM = 64
RWKV_HEADS = D_MODEL // RWKV_HEAD_DIM
DECAY_LORA = 64
ICLR_LORA = 64
GATE_LORA = 160
D_FF = 2816
CONV_W = 3
ALPHA = (2.0 * DEPTH) ** 0.25
BETA = (8.0 * DEPTH) ** -0.25
LN_EPS = 1e-5
QK_EPS = 1e-6
GN_EPS = 64e-5
ATTN_SCALE = FOX_HEAD_DIM ** -0.5

kernel_name = 'fox_rwkv7_convglu_deepnorm_step'


def layer_norm(x, g, b):
    xf = x.astype(jnp.float32)
    mu = jnp.mean(xf, -1, keepdims=True)
    var = jnp.mean(jnp.square(xf - mu), -1, keepdims=True)
    return ((xf - mu) * lax.rsqrt(var + LN_EPS) * g + b).astype(x.dtype)


def rms_norm(x, g):
    xf = x.astype(jnp.float32)
    return (xf * lax.rsqrt(jnp.mean(xf * xf, -1, keepdims=True) + QK_EPS) * g).astype(x.dtype)


def fox_project(x, w_in, b_f, g_q, g_k):
    B, T, _ = x.shape
    h = x @ w_in
    q, k, v, og, fl = jnp.split(h, [D_MODEL, 2 * D_MODEL, 3 * D_MODEL, 4 * D_MODEL], axis=-1)
    shp = (B, T, FOX_HEADS, FOX_HEAD_DIM)
    q = rms_norm(q.reshape(shp), g_q)
    k = rms_norm(k.reshape(shp), g_k)
    v = v.reshape(shp)
    logf = jax.nn.log_sigmoid((fl + b_f).astype(jnp.float32))
    return q, k, v, og, logf


def fox_attend(q, cq, k, v, ckT, q_start):
    tq, tk = q.shape[1], k.shape[1]
    s = jnp.einsum('bqhd,bkhd->bhqk', q, k, preferred_element_type=jnp.float32) * ATTN_SCALE
    s = s + cq.transpose(0, 2, 1)[..., None] - ckT[:, :, None, :]
    qpos = q_start + jnp.arange(tq)
    kpos = jnp.arange(tk)
    s = jnp.where(kpos[None, :] <= qpos[:, None], s, -jnp.inf)
    p = jax.nn.softmax(s, axis=-1).astype(v.dtype)
    return jnp.einsum('bhqk,bkhd->bqhd', p, v)


def fox_out(o, og, w_out):
    B, T = o.shape[:2]
    return (o.reshape(B, T, D_MODEL) * jax.nn.sigmoid(og)) @ w_out


def fox_prompt(x, w_in, b_f, g_q, g_k, w_out):
    B, S, _ = x.shape
    q, k, v, og, logf = fox_project(x, w_in, b_f, g_q, g_k)
    c = jnp.cumsum(logf, axis=1)
    ckT = c.transpose(0, 2, 1)
    nb = S // Q_BLOCK
    qb = q.reshape(B, nb, Q_BLOCK, FOX_HEADS, FOX_HEAD_DIM).transpose(1, 0, 2, 3, 4)
    cb = c.reshape(B, nb, Q_BLOCK, FOX_HEADS).transpose(1, 0, 2, 3)

    def block(args):
        qi, ci, start = args
        return fox_attend(qi, ci, k, v, ckT, start)

    o = lax.map(block, (qb, cb, jnp.arange(nb) * Q_BLOCK))
    o = o.transpose(1, 0, 2, 3, 4).reshape(B, S, FOX_HEADS, FOX_HEAD_DIM)
    return fox_out(o, og, w_out), k, v, logf


def fox_sample(x, cache_k, cache_v, cache_logf, page_table, w_in, b_f, g_q, g_k, w_out):
    B = x.shape[0]
    q, k, v, og, logf = fox_project(x, w_in, b_f, g_q, g_k)
    past = page_table.shape[1] * PAGE_SIZE
    k_past = cache_k[page_table].reshape(B, past, FOX_HEADS, FOX_HEAD_DIM)
    v_past = cache_v[page_table].reshape(B, past, FOX_HEADS, FOX_HEAD_DIM)
    lf_past = cache_logf[page_table].reshape(B, past, FOX_HEADS).astype(jnp.float32)
    k_all = jnp.concatenate([k_past, k.astype(k_past.dtype)], axis=1)
    v_all = jnp.concatenate([v_past, v.astype(v_past.dtype)], axis=1)
    c_all = jnp.cumsum(jnp.concatenate([lf_past, logf], axis=1), axis=1)
    o = fox_attend(q, c_all[:, past:], k_all, v_all, c_all.transpose(0, 2, 1), past)
    return fox_out(o, og, w_out), k, v, logf


def rwkv_step(S, inp):
    r, w, k, v, kk, a = inp
    sa = jnp.einsum('bhij,bhj->bhi', S, -kk)
    S = S * w[:, :, None, :] + sa[..., None] * (kk * a)[:, :, None, :] + v[..., None] * k[:, :, None, :]
    return S, jnp.einsum('bhij,bhj->bhi', S, r)


def rwkv_mixer(x, x_prev, S0, mu, w_rkv, w0, decay_A, decay_B, a0, iclr_A, iclr_B,
               gate_A, gate_B, k_k, k_a, r_k, gn_g, gn_b, w_out):
    B, T, _ = x.shape
    hs = (B, T, RWKV_HEADS, RWKV_HEAD_DIM)
    f32 = jnp.float32
    xx = jnp.concatenate([x_prev[:, None].astype(x.dtype), x[:, :-1]], axis=1) - x
    xr, xw, xk, xv, xa, xg = x[None] + xx[None] * mu[:, None, None, :]
    r, k, v = jnp.einsum('gbtd,gde->gbte', jnp.stack([xr, xk, xv]), w_rkv)
    logw = -jax.nn.softplus(-(w0 + jnp.tanh(xw @ decay_A) @ decay_B).astype(f32)) - 0.5
    w = jnp.exp(-jnp.exp(logw))
    a = jax.nn.sigmoid((a0 + (xa @ iclr_A) @ iclr_B).astype(f32))
    g = jax.nn.sigmoid(xg @ gate_A) @ gate_B
    kk = (k * k_k).reshape(hs).astype(f32)
    kk = kk * lax.rsqrt(jnp.maximum(jnp.sum(kk * kk, -1, keepdims=True), 1e-24))
    k = k.astype(f32) * (1.0 + (a - 1.0) * k_a)
    rh, wh, kh, vh, ah = [z.reshape(hs).astype(f32) for z in (r, w, k, v, a)]
    tm = lambda z: z.transpose(1, 0, 2, 3)
    S_fin, y = lax.scan(rwkv_step, S0.astype(f32), (tm(rh), tm(wh), tm(kh), tm(vh), tm(kk), tm(ah)))
    y = tm(y)
    ym = jnp.mean(y, -1, keepdims=True)
    yv = jnp.mean(jnp.square(y - ym), -1, keepdims=True)
    yn = ((y - ym) * lax.rsqrt(yv + GN_EPS)).reshape(B, T, D_MODEL) * gn_g + gn_b
    bonus = (jnp.sum(rh * kh * r_k, -1, keepdims=True) * vh).reshape(B, T, D_MODEL)
    out = ((yn + bonus) * g).astype(x.dtype) @ w_out
    return out, S_fin, x[:, -1]


def conv_glu(x, conv_buf, w_in, conv_w, conv_b, w_out):
    u, gt = jnp.split(x @ w_in, 2, axis=-1)
    gpad = jnp.concatenate([conv_buf.astype(gt.dtype), gt], axis=1)
    gc = lax.conv_general_dilated(gpad, conv_w[:, None, :].astype(gpad.dtype), (1,), 'VALID',
                                  dimension_numbers=('NWC', 'WIO', 'NWC'),
                                  feature_group_count=D_FF) + conv_b
    y = (u * jax.nn.gelu(gc)) @ w_out
    return y, gpad[:, -(CONV_W - 1):]


def setup_inputs(seed: int = 0) -> dict:
    key = jax.random.key(seed)
    ks = iter(jax.random.split(key, 48))
    f32 = jnp.float32
    nrm = lambda shape, scale: jax.random.normal(next(ks), shape, f32) * scale
    D, H, HD, F = D_MODEL, FOX_HEADS, FOX_HEAD_DIM, D_FF
    n_pages = PAST_LEN // PAGE_SIZE
    n_used = DEC_BATCH * n_pages
    n_pool = n_used + max(1, n_used // 4)
    page_table = jax.random.permutation(next(ks), n_pool)[:n_used].reshape(DEC_BATCH, n_pages).astype(jnp.int32)
    fox_b_f = jnp.linspace(0.5, 6.0, H).astype(f32) + nrm((H,), 0.1)
    inputs = {
        'x_prompt': nrm((BATCH, SEQ, D), 1.0),
        'x_sample': nrm((DEC_BATCH, DEC_SEQ, D), 1.0),
        'cache_k': nrm((n_pool, PAGE_SIZE, H, HD), 1.0),
        'cache_v': nrm((n_pool, PAGE_SIZE, H, HD), 0.5),
        'cache_logf': jax.nn.log_sigmoid(fox_b_f + nrm((n_pool, PAGE_SIZE, H), 1.0)),
        'page_table': page_table,
        'state_wkv': nrm((DEC_BATCH, RWKV_HEADS, RWKV_HEAD_DIM, RWKV_HEAD_DIM), 0.5),
        'state_shift': nrm((DEC_BATCH, D), 1.0),
        'state_conv': nrm((DEPTH, DEC_BATCH, CONV_W - 1, F), 0.5),
        'fox_w_in': jnp.concatenate([nrm((D, 2 * D), D ** -0.5), nrm((D, D), BETA * D ** -0.5),
                                     nrm((D, D), D ** -0.5), nrm((D, H), D ** -0.5)], axis=1),
        'fox_b_f': fox_b_f,
        'fox_g_q': 1.0 + nrm((HD,), 0.02),
        'fox_g_k': 1.0 + nrm((HD,), 0.02),
        'fox_w_out': nrm((D, D), BETA * D ** -0.5),
        'rwkv_mu': jax.random.uniform(next(ks), (6, D), f32),
        'rwkv_w_rkv': nrm((3, D, D), D ** -0.5) * jnp.array([1.0, 1.0, BETA], f32)[:, None, None],
        'rwkv_w0': nrm((D,), 0.5),
        'rwkv_decay_A': nrm((D, DECAY_LORA), D ** -0.5),
        'rwkv_decay_B': nrm((DECAY_LORA, D), DECAY_LORA ** -0.5),
        'rwkv_a0': nrm((D,), 0.1),
        'rwkv_iclr_A': nrm((D, ICLR_LORA), D ** -0.5),
        'rwkv_iclr_B': nrm((ICLR_LORA, D), ICLR_LORA ** -0.5),
        'rwkv_gate_A': nrm((D, GATE_LORA), D ** -0.5),
        'rwkv_gate_B': nrm((GATE_LORA, D), GATE_LORA ** -0.5),
        'rwkv_k_k': 0.85 + nrm((D,), 0.05),
        'rwkv_k_a': 1.0 + nrm((D,), 0.05),
        'rwkv_r_k': nrm((RWKV_HEADS, RWKV_HEAD_DIM), 0.1),
        'rwkv_gn_g': 1.0 + nrm((D,), 0.02),
        'rwkv_gn_b': nrm((D,), 0.02),
        'rwkv_w_out': nrm((D, D), BETA * D ** -0.5),
        'ffn_w_in': nrm((DEPTH, D, 2 * F), BETA * D ** -0.5),
        'ffn_conv_w': nrm((DEPTH, CONV_W, F), CONV_W ** -0.5),
        'ffn_conv_b': nrm((DEPTH, F), 0.02),
        'ffn_w_out': nrm((DEPTH, F, D), BETA * F ** -0.5),
        'ln1_g': 1.0 + nrm((DEPTH, D), 0.02),
        'ln1_b': nrm((DEPTH, D), 0.02),
        'ln2_g': 1.0 + nrm((DEPTH, D), 0.02),
        'ln2_b': nrm((DEPTH, D), 0.02),
    }
    return inputs


def reference(x_prompt, x_sample, cache_k, cache_v, cache_logf, page_table, state_wkv, state_shift,
              state_conv, fox_w_in, fox_b_f, fox_g_q, fox_g_k, fox_w_out, rwkv_mu, rwkv_w_rkv,
              rwkv_w0, rwkv_decay_A, rwkv_decay_B, rwkv_a0, rwkv_iclr_A, rwkv_iclr_B, rwkv_gate_A,
              rwkv_gate_B, rwkv_k_k, rwkv_k_a, rwkv_r_k, rwkv_gn_g, rwkv_gn_b, rwkv_w_out,
              ffn_w_in, ffn_conv_w, ffn_conv_b, ffn_w_out, ln1_g, ln1_b, ln2_g, ln2_b):
    xp, xs = x_prompt, x_sample
    bp = xp.shape[0]
    fox = (fox_w_in, fox_b_f, fox_g_q, fox_g_k, fox_w_out)
    rwkv = (rwkv_mu, rwkv_w_rkv, rwkv_w0, rwkv_decay_A, rwkv_decay_B, rwkv_a0, rwkv_iclr_A,
            rwkv_iclr_B, rwkv_gate_A, rwkv_gate_B, rwkv_k_k, rwkv_k_a, rwkv_r_k, rwkv_gn_g,
            rwkv_gn_b, rwkv_w_out)
    conv_p, conv_s = [], []
    for i in range(DEPTH):
        if i % N_MIXERS == 0:
            mp, k_p, v_p, lf_p = fox_prompt(xp, *fox)
            ms, k_s, v_s, lf_s = fox_sample(xs, cache_k, cache_v, cache_logf, page_table, *fox)
        else:
            zero_shift = jnp.zeros((bp, D_MODEL), xp.dtype)
            zero_wkv = jnp.zeros((bp, RWKV_HEADS, RWKV_HEAD_DIM, RWKV_HEAD_DIM), jnp.float32)
            mp, wkv_p, shift_p = rwkv_mixer(xp, zero_shift, zero_wkv, *rwkv)
            ms, wkv_s, shift_s = rwkv_mixer(xs, state_shift, state_wkv, *rwkv)
        xp = layer_norm(ALPHA * xp + mp, ln1_g[i], ln1_b[i])
        xs = layer_norm(ALPHA * xs + ms, ln1_g[i], ln1_b[i])
        fp, cp = conv_glu(xp, jnp.zeros((bp, CONV_W - 1, D_FF), xp.dtype),
                          ffn_w_in[i], ffn_conv_w[i], ffn_conv_b[i], ffn_w_out[i])
        fs, cs = conv_glu(xs, state_conv[i], ffn_w_in[i], ffn_conv_w[i], ffn_conv_b[i], ffn_w_out[i])
        conv_p.append(cp)
        conv_s.append(cs.astype(state_conv.dtype))
        xp = layer_norm(ALPHA * xp + fp, ln2_g[i], ln2_b[i])
        xs = layer_norm(ALPHA * xs + fs, ln2_g[i], ln2_b[i])
    return (xp, xs,
            k_p, v_p, lf_p.astype(xp.dtype),
            k_s.astype(cache_k.dtype), v_s.astype(cache_v.dtype), lf_s.astype(cache_logf.dtype),
            wkv_p.astype(xp.dtype), shift_p,
            wkv_s.astype(state_wkv.dtype), shift_s.astype(state_shift.dtype),
            jnp.stack(conv_p), jnp.stack(conv_s))
```

```python
import functools
import math

import jax
import jax.numpy as jnp
from jax import lax
from jax.experimental import pallas as pl
from jax.experimental.pallas import tpu as pltpu

F32 = jnp.float32
BF16 = jnp.bfloat16

LANES = 128
HEAD_DIM = 64
PAIR = 2 * HEAD_DIM
DEPTH = 2
ALPHA = (2.0 * DEPTH) ** 0.25
LN_EPS = 1e-5
QK_EPS = 1e-6
GN_EPS = 64e-5
ATTN_SCALE = HEAD_DIM ** -0.5
VMEM_LIMIT = 56 * 1024 * 1024
NEG_INF = float("-inf")


def _cparams(*sem):
    return pltpu.CompilerParams(dimension_semantics=sem, vmem_limit_bytes=VMEM_LIMIT)


def _iota(shape, axis):
    return lax.broadcasted_iota(jnp.int32, shape, axis)


def _pair_ones():
    r = jnp.right_shift(_iota((PAIR, PAIR), 0), 6)
    c = jnp.right_shift(_iota((PAIR, PAIR), 1), 6)
    return jnp.where(r == c, 1.0, 0.0).astype(BF16)


def _split2(x):
    hi = x.astype(BF16)
    lo = (x - hi.astype(F32)).astype(BF16)
    return hi, lo


def _split3(x):
    hi = x.astype(BF16)
    r1 = x - hi.astype(F32)
    mid = r1.astype(BF16)
    lo = (r1 - mid.astype(F32)).astype(BF16)
    return hi, mid, lo


def _dot(a, b):
    return jnp.dot(a, b, preferred_element_type=F32)


def _dot_nt(a, b):
    return lax.dot_general(a, b, (((1,), (1,)), ((), ())), preferred_element_type=F32)


def _bdot(a, b):
    return _dot(a.astype(BF16), b.astype(BF16))


def _head_sum(x, e2):
    hi, lo = _split2(x)
    return _dot(hi, e2) + _dot(lo, e2)


def _head_sum_wide(x, e2):
    cols = [_head_sum(x[:, c * PAIR:(c + 1) * PAIR], e2) for c in range(x.shape[1] // PAIR)]
    return jnp.concatenate(cols, axis=1)


def _sigmoid(z):
    return 1.0 / (1.0 + jnp.exp(-z))


def _log_sigmoid(z):
    return jnp.minimum(z, 0.0) - jnp.log(1.0 + jnp.exp(-jnp.abs(z)))


def _layer_norm(y, g, b):
    mu = jnp.mean(y, axis=-1, keepdims=True)
    yc = y - mu
    var = jnp.mean(yc * yc, axis=-1, keepdims=True)
    return yc * lax.rsqrt(var + LN_EPS) * g + b


def _row_tile(m, want):
    t = min(m, want)
    assert m % t == 0, (m, t)
    return t


def _fox_proj_kernel(x_ref, w_ref, wf_ref, bf_ref, gq_ref, gk_ref,
                     q_ref, k_ref, v_ref, kb_ref, vb_ref, gate_ref, lf_ref):
    d = x_ref.shape[1]
    xb = x_ref[...].astype(BF16)
    e2 = _pair_ones()

    def rms(h, g_ref):
        ss = _head_sum_wide(h * h, e2)
        return h * lax.rsqrt(ss * (1.0 / HEAD_DIM) + QK_EPS) * g_ref[...]

    q = rms(_dot(xb, w_ref[:, 0:d]), gq_ref)
    q_ref[...] = (q * ATTN_SCALE).astype(BF16)
    k = rms(_dot(xb, w_ref[:, d:2 * d]), gk_ref)
    k_ref[...] = k
    kb_ref[...] = k.astype(BF16)
    v = _dot(xb, w_ref[:, 2 * d:3 * d])
    v_ref[...] = v
    vb_ref[...] = v.astype(BF16)
    gate_ref[...] = _sigmoid(_dot(xb, w_ref[:, 3 * d:4 * d]))
    lf_ref[...] = _log_sigmoid(_dot(xb, wf_ref[...]) + bf_ref[...])


def _fox_proj(x, w_main, w_f, b_f, g_q, g_k, tm):
    m, d = x.shape
    tm = _row_tile(m, tm)
    row = lambda i: (i, 0)
    fixed = lambda i: (0, 0)
    out_shape = (
        jax.ShapeDtypeStruct((m, d), BF16),
        jax.ShapeDtypeStruct((m, d), F32),
        jax.ShapeDtypeStruct((m, d), F32),
        jax.ShapeDtypeStruct((m, d), BF16),
        jax.ShapeDtypeStruct((m, d), BF16),
        jax.ShapeDtypeStruct((m, d), F32),
        jax.ShapeDtypeStruct((m, LANES), F32),
    )
    return pl.pallas_call(
        _fox_proj_kernel,
        grid=(m // tm,),
        in_specs=[
            pl.BlockSpec((tm, d), row),
            pl.BlockSpec((d, 4 * d), fixed),
            pl.BlockSpec((d, LANES), fixed),
            pl.BlockSpec((1, LANES), fixed),
            pl.BlockSpec((1, d), fixed),
            pl.BlockSpec((1, d), fixed),
        ],
        out_specs=[pl.BlockSpec((tm, d), row)] * 6 + [pl.BlockSpec((tm, LANES), row)],
        out_shape=out_shape,
        name="fox_proj",
        compiler_params=_cparams("arbitrary"),
    )(x, w_main, w_f, b_f, g_q, g_k)


def _cumsum_kernel(lf_ref, c_ref, *, chunk):
    s = lf_ref.shape[1]
    r = _iota((chunk, chunk), 0)
    c = _iota((chunk, chunk), 1)
    tri = jnp.where(c <= r, 1.0, 0.0).astype(BF16)
    carry = jnp.zeros((1, LANES), F32)
    for i in range(s // chunk):
        x = lf_ref[0, i * chunk:(i + 1) * chunk, :]
        hi, mid, lo = _split3(x)
        cs = _dot(tri, hi) + _dot(tri, mid) + _dot(tri, lo) + carry
        c_ref[0, i * chunk:(i + 1) * chunk, :] = cs
        carry = cs[chunk - 1:chunk, :]


def _cumsum_seq(lf):
    b, s, _ = lf.shape
    chunk = min(s, 256)
    spec = pl.BlockSpec((1, s, LANES), lambda i: (i, 0, 0))
    return pl.pallas_call(
        functools.partial(_cumsum_kernel, chunk=chunk),
        grid=(b,),
        in_specs=[spec],
        out_specs=spec,
        out_shape=jax.ShapeDtypeStruct(lf.shape, F32),
        name="logf_cumsum",
        compiler_params=_cparams("arbitrary"),
    )(lf)


def _fox_attn_kernel(q_ref, k_ref, v_ref, cq_ref, ck_ref, o_ref, m_sc, l_sc, acc_sc, *, blk):
    seq = q_ref.shape[1]
    nblk = seq // blk
    lane = _iota((1, PAIR), 1)
    row = _iota((blk, blk), 0)
    col = _iota((blk, blk), 1)
    causal = col <= row

    def q_body(qi, carry):
        q0 = pl.multiple_of(qi * blk, blk)
        qp = q_ref[0, pl.ds(q0, blk), :]
        outs = []
        for j in range(2):
            hmask = (lane >= HEAD_DIM) if j else (lane < HEAD_DIM)
            qh = jnp.where(hmask, qp, jnp.zeros_like(qp))
            cq = cq_ref[0, 0, pl.ds(q0, blk), j:j + 1]
            m_sc[...] = jnp.full(m_sc.shape, NEG_INF, F32)
            l_sc[...] = jnp.zeros(l_sc.shape, F32)
            acc_sc[...] = jnp.zeros(acc_sc.shape, F32)

            def kv_step(ki, masked):
                k0 = pl.multiple_of(ki * blk, blk)
                kb = k_ref[0, pl.ds(k0, blk), :]
                vb = v_ref[0, pl.ds(k0, blk), :]
                ck = ck_ref[0, 0, ki, j:j + 1, :]
                s = _dot_nt(qh, kb) + (cq - ck)
                if masked:
                    s = jnp.where(causal, s, NEG_INF)
                m_old = m_sc[...]
                m_new = jnp.maximum(m_old, jnp.max(s, axis=-1, keepdims=True))
                p = jnp.exp(s - m_new)
                alpha = jnp.exp(m_old - m_new)
                l_sc[...] = alpha * l_sc[...] + jnp.sum(p, axis=-1, keepdims=True)
                acc_sc[...] = alpha * acc_sc[...] + _dot(p.astype(BF16), vb)
                m_sc[...] = m_new

            def kv_body(ki, c):
                kv_step(ki, False)
                return c

            lax.fori_loop(0, qi, kv_body, 0)
            kv_step(qi, True)
            outs.append(acc_sc[...] / l_sc[...])
        o_ref[0, pl.ds(q0, blk), :] = jnp.where(lane < HEAD_DIM, outs[0], outs[1])
        return carry

    lax.fori_loop(0, nblk, q_body, 0)


def _fox_attention(q, kb, vb, c, blk):
    b, s, d = q.shape
    hp = d // PAIR
    blk = min(blk, s)
    nblk = s // blk
    c_heads = c[:, :, :2 * hp].reshape(b, s, hp, 2)
    cq = c_heads.transpose(0, 2, 1, 3)
    ck = c_heads.reshape(b, nblk, blk, hp, 2).transpose(0, 3, 1, 4, 2)
    seq_spec = pl.BlockSpec((1, s, PAIR), lambda i, j: (i, 0, j))
    return pl.pallas_call(
        functools.partial(_fox_attn_kernel, blk=blk),
        grid=(b, hp),
        in_specs=[
            seq_spec, seq_spec, seq_spec,
            pl.BlockSpec((1, 1, s, 2), lambda i, j: (i, j, 0, 0)),
            pl.BlockSpec((1, 1, nblk, 2, blk), lambda i, j: (i, j, 0, 0, 0)),
        ],
        out_specs=seq_spec,
        out_shape=jax.ShapeDtypeStruct((b, s, d), F32),
        name="fox_attn",
        scratch_shapes=[
            pltpu.VMEM((blk, 1), F32),
            pltpu.VMEM((blk, 1), F32),
            pltpu.VMEM((blk, PAIR), F32),
        ],
        compiler_params=_cparams("arbitrary", "arbitrary"),
    )(q, kb, vb, cq, ck)


def _fox_decode_kernel(pt_ref, q_ref, kn_ref, vn_ref, lfn_ref, kpg_ref, vpg_ref, lfpg_ref, o_ref,
                       qa_sc, m_sc, l_sc, acc_sc, carry_sc, *, heads):
    p = pl.program_id(1)
    npages = pl.num_programs(1)
    page = kpg_ref.shape[1]
    d = kpg_ref.shape[2]
    hrows = m_sc.shape[0]
    hm = jnp.right_shift(_iota((hrows, d), 1), 6) == _iota((hrows, d), 0)
    flat = page * heads
    hsel = jnp.bitwise_and(_iota((hrows, flat), 1), heads - 1) == _iota((hrows, flat), 0)

    @pl.when(p == 0)
    def _init():
        q = q_ref[0].astype(F32)
        qa_sc[...] = jnp.where(hm, q, 0.0).astype(BF16)
        s_self = jnp.sum(jnp.where(hm, q * kn_ref[0], 0.0), axis=-1, keepdims=True)
        m_sc[...] = s_self
        l_sc[...] = jnp.ones(l_sc.shape, F32)
        acc_sc[...] = jnp.broadcast_to(vn_ref[0], acc_sc.shape)
        eye = _iota((hrows, LANES), 1) == _iota((hrows, LANES), 0)
        carry_sc[...] = jnp.sum(jnp.where(eye, lfn_ref[0], 0.0), axis=-1, keepdims=True)

    kb = kpg_ref[0].astype(BF16)
    vb = vpg_ref[0].astype(BF16)
    s = _dot_nt(qa_sc[...], kb)
    lf = lfpg_ref[0]
    lfh = jnp.where(hsel, lf, 0.0)
    tok = jnp.right_shift(_iota((page, flat), 1), int(math.log2(heads)))
    later = jnp.where(tok > _iota((page, flat), 0), 1.0, 0.0).astype(BF16)
    hi, mid, lo = _split3(lfh)
    suffix = _dot_nt(hi, later) + _dot_nt(mid, later) + _dot_nt(lo, later)
    carry = carry_sc[...]
    s = s + (carry + suffix)
    carry_sc[...] = carry + jnp.sum(lfh, axis=-1, keepdims=True)

    m_old = m_sc[...]
    m_new = jnp.maximum(m_old, jnp.max(s, axis=-1, keepdims=True))
    pr = jnp.exp(s - m_new)
    alpha = jnp.exp(m_old - m_new)
    l_sc[...] = alpha * l_sc[...] + jnp.sum(pr, axis=-1, keepdims=True)
    acc_sc[...] = alpha * acc_sc[...] + _dot(pr.astype(BF16), vb)
    m_sc[...] = m_new

    @pl.when(p == npages - 1)
    def _fin():
        o = jnp.where(hm, acc_sc[...] / l_sc[...], 0.0)
        o_ref[0] = jnp.sum(o, axis=0, keepdims=True)


def _fox_decode(q, k_new, v_new, lf_new, cache_k, cache_v, cache_logf, page_table):
    bd, d = q.shape
    n_pool, page, heads, hd = cache_k.shape
    assert hd == HEAD_DIM and heads * hd == d and heads & (heads - 1) == 0
    npages = page_table.shape[1]
    hrows = max(heads, 8)
    ck = cache_k.reshape(n_pool, page, d)
    cv = cache_v.reshape(n_pool, page, d)
    clf = cache_logf.reshape(n_pool, 1, page * heads)
    per_req = lambda w: pl.BlockSpec((1, 1, w), lambda b, p, pt: (b, 0, 0))
    paged = lambda shp: pl.BlockSpec(shp, lambda b, p, pt: (pt[b, npages - 1 - p], 0, 0))
    grid_spec = pltpu.PrefetchScalarGridSpec(
        num_scalar_prefetch=1,
        grid=(bd, npages),
        in_specs=[per_req(d), per_req(d), per_req(d), per_req(LANES),
                  paged((1, page, d)), paged((1, page, d)), paged((1, 1, page * heads))],
        out_specs=per_req(d),
        scratch_shapes=[
            pltpu.VMEM((hrows, d), BF16),
            pltpu.VMEM((hrows, 1), F32),
            pltpu.VMEM((hrows, 1), F32),
            pltpu.VMEM((hrows, d), F32),
            pltpu.VMEM((hrows, 1), F32),
        ],
    )
    o = pl.pallas_call(
        functools.partial(_fox_decode_kernel, heads=heads),
        grid_spec=grid_spec,
        out_shape=jax.ShapeDtypeStruct((bd, 1, d), F32),
        name="fox_decode",
        compiler_params=_cparams("arbitrary", "arbitrary"),
    )(page_table, q.reshape(bd, 1, d), k_new.reshape(bd, 1, d), v_new.reshape(bd, 1, d),
      lf_new.reshape(bd, 1, LANES), ck, cv, clf)
    return o.reshape(bd, d)


def _gated_out_ln_kernel(a_ref, gate_ref, x_ref, w_ref, g_ref, b_ref, o_ref):
    h = (a_ref[...] * gate_ref[...]).astype(BF16)
    y = ALPHA * x_ref[...] + _dot(h, w_ref[...])
    o_ref[...] = _layer_norm(y, g_ref[...], b_ref[...])


def _gated_out_ln(a, gate, x, w_out, ln_g, ln_b, tm):
    m, d = x.shape
    tm = _row_tile(m, tm)
    row = pl.BlockSpec((tm, d), lambda i: (i, 0))
    vec = pl.BlockSpec((1, d), lambda i: (0, 0))
    return pl.pallas_call(
        _gated_out_ln_kernel,
        grid=(m // tm,),
        in_specs=[row, row, row, pl.BlockSpec((d, d), lambda i: (0, 0)), vec, vec],
        out_specs=row,
        out_shape=jax.ShapeDtypeStruct((m, d), F32),
        name="gated_out_ln",
        compiler_params=_cparams("arbitrary"),
    )(a, gate, x, w_out, ln_g, ln_b)


def _gelu_tanh(x):
    c = math.sqrt(2.0 / math.pi)
    return 0.5 * x * (1.0 + jnp.tanh(c * (x + 0.044715 * (x * x * x))))


def _ffn_seq_kernel(x_ref, wu_ref, wg_ref, cw_ref, cb_ref, wo_ref, g_ref, b_ref,
                    o_ref, tail_ref, acc_sc, carry_sc, *, tiles_per_seq):
    i = pl.program_id(0)
    j = pl.program_id(1)
    tm = x_ref.shape[0]
    xb = x_ref[...].astype(BF16)
    u = _dot(xb, wu_ref[...])
    gt = _dot(xb, wg_ref[...])
    @pl.when(i % tiles_per_seq == 0)
    def _seq_start():
        carry_sc[j] = jnp.zeros(carry_sc.shape[1:], F32)

    prev = carry_sc[j]
    rows = _iota(gt.shape, 0)
    g1 = jnp.where(rows == 0, prev[7:8, :], pltpu.roll(gt, 1, 0))
    g2 = jnp.where(rows == 0, prev[6:7, :], jnp.where(rows == 1, prev[7:8, :], pltpu.roll(gt, 2, 0)))
    gc = cw_ref[0:1, :] * g2 + cw_ref[1:2, :] * g1 + cw_ref[2:3, :] * gt + cb_ref[...]
    carry_sc[j] = gt[tm - 8:tm, :]
    tail_ref[0, j] = gt[tm - 2:tm, :]
    part = _dot((u * _gelu_tanh(gc)).astype(BF16), wo_ref[...])

    @pl.when(j == 0)
    def _first():
        acc_sc[...] = part

    @pl.when(j > 0)
    def _rest():
        acc_sc[...] += part

    @pl.when(j == pl.num_programs(1) - 1)
    def _fin():
        y = ALPHA * x_ref[...] + acc_sc[...]
        o_ref[...] = _layer_norm(y, g_ref[...], b_ref[...])


def _ffn_step_kernel(x_ref, p2_ref, p1_ref, wu_ref, wg_ref, cw_ref, cb_ref, wo_ref, g_ref, b_ref,
                     o_ref, gt_ref, acc_sc):
    j = pl.program_id(1)
    xb = x_ref[...].astype(BF16)
    u = _dot(xb, wu_ref[...])
    gt = _dot(xb, wg_ref[...])
    gc = cw_ref[0:1, :] * p2_ref[...] + cw_ref[1:2, :] * p1_ref[...] + cw_ref[2:3, :] * gt + cb_ref[...]
    gt_ref[...] = gt
    part = _dot((u * _gelu_tanh(gc)).astype(BF16), wo_ref[...])

    @pl.when(j == 0)
    def _first():
        acc_sc[...] = part

    @pl.when(j > 0)
    def _rest():
        acc_sc[...] += part

    @pl.when(j == pl.num_programs(1) - 1)
    def _fin():
        y = ALPHA * x_ref[...] + acc_sc[...]
        o_ref[...] = _layer_norm(y, g_ref[...], b_ref[...])


def _ffn_chunk(f):
    for nf in (1, 2, 4, 11, 22):
        if f % nf == 0 and (f // nf) % LANES == 0 and f // nf <= 1408:
            return f // nf
    return LANES


def _ffn_seq(x, seq, w_in, conv_w, conv_b, w_out, ln_g, ln_b, tm):
    m, d = x.shape
    f = w_out.shape[0]
    tf = _ffn_chunk(f)
    nf = f // tf
    tm = _row_tile(seq, tm)
    tps = seq // tm
    y, tail = pl.pallas_call(
        functools.partial(_ffn_seq_kernel, tiles_per_seq=tps),
        grid=(m // tm, nf),
        in_specs=[
            pl.BlockSpec((tm, d), lambda i, j: (i, 0)),
            pl.BlockSpec((d, tf), lambda i, j: (0, j)),
            pl.BlockSpec((d, tf), lambda i, j: (0, nf + j)),
            pl.BlockSpec((3, tf), lambda i, j: (0, j)),
            pl.BlockSpec((1, tf), lambda i, j: (0, j)),
            pl.BlockSpec((tf, d), lambda i, j: (j, 0)),
            pl.BlockSpec((1, d), lambda i, j: (0, 0)),
            pl.BlockSpec((1, d), lambda i, j: (0, 0)),
        ],
        out_specs=[
            pl.BlockSpec((tm, d), lambda i, j: (i, 0)),
            pl.BlockSpec((1, nf, 2, tf), lambda i, j: (i // tps, 0, 0, 0)),
        ],
        out_shape=(jax.ShapeDtypeStruct((m, d), F32), jax.ShapeDtypeStruct((m // seq, nf, 2, tf), F32)),
        name="ffn_seq",
        scratch_shapes=[pltpu.VMEM((tm, d), F32), pltpu.VMEM((nf, 8, tf), F32)],
        compiler_params=_cparams("arbitrary", "arbitrary"),
    )(x, w_in, w_in, conv_w, conv_b, w_out, ln_g, ln_b)
    return y, tail.transpose(0, 2, 1, 3).reshape(m // seq, 2, f)


def _ffn_step(x, prev2, prev1, w_in, conv_w, conv_b, w_out, ln_g, ln_b):
    m, d = x.shape
    f = w_out.shape[0]
    tf = _ffn_chunk(f)
    nf = f // tf
    return pl.pallas_call(
        _ffn_step_kernel,
        grid=(1, nf),
        in_specs=[
            pl.BlockSpec((m, d), lambda i, j: (0, 0)),
            pl.BlockSpec((m, tf), lambda i, j: (0, j)),
            pl.BlockSpec((m, tf), lambda i, j: (0, j)),
            pl.BlockSpec((d, tf), lambda i, j: (0, j)),
            pl.BlockSpec((d, tf), lambda i, j: (0, nf + j)),
            pl.BlockSpec((3, tf), lambda i, j: (0, j)),
            pl.BlockSpec((1, tf), lambda i, j: (0, j)),
            pl.BlockSpec((tf, d), lambda i, j: (j, 0)),
            pl.BlockSpec((1, d), lambda i, j: (0, 0)),
            pl.BlockSpec((1, d), lambda i, j: (0, 0)),
        ],
        out_specs=[
            pl.BlockSpec((m, d), lambda i, j: (0, 0)),
            pl.BlockSpec((m, tf), lambda i, j: (0, j)),
        ],
        out_shape=(jax.ShapeDtypeStruct((m, d), F32), jax.ShapeDtypeStruct((m, f), F32)),
        name="ffn_step",
        scratch_shapes=[pltpu.VMEM((m, d), F32)],
        compiler_params=_cparams("arbitrary", "arbitrary"),
    )(x, prev2, prev1, w_in, w_in, conv_w, conv_b, w_out, ln_g, ln_b)


def _rwkv_prep_kernel(x_ref, xp_ref, mu_ref, wr_ref, wk_ref, wv_ref, w0_ref, da_ref, db_ref,
                      a0_ref, ia_ref, ib_ref, ga_ref, gb_ref, kk_ref, ka_ref,
                      r_out, lw_out, k_out, v_out, kk_out, b_out, g_out):
    x = x_ref[...]
    xx = xp_ref[...] - x
    mix = lambda n: (x + xx * mu_ref[n:n + 1, :]).astype(BF16)
    xr, xw, xk, xv, xa, xg = (mix(n) for n in range(6))
    r = _dot(xr, wr_ref[...])
    k = _dot(xk, wk_ref[...])
    v = _dot(xv, wv_ref[...])
    dw = _bdot(jnp.tanh(_dot(xw, da_ref[...])), db_ref[...])
    z = w0_ref[...] + dw
    lw_out[...] = -jnp.exp(_log_sigmoid(z) - 0.5)
    a = _sigmoid(a0_ref[...] + _bdot(_dot(xa, ia_ref[...]), ib_ref[...]))
    g_out[...] = _bdot(_sigmoid(_dot(xg, ga_ref[...])), gb_ref[...])
    kk = k * kk_ref[...]
    ss = _head_sum_wide(kk * kk, _pair_ones())
    kk = kk * lax.rsqrt(jnp.maximum(ss, 1e-24))
    r_out[...] = r
    v_out[...] = v
    kk_out[...] = kk
    b_out[...] = kk * a
    k_out[...] = k * (1.0 + (a - 1.0) * ka_ref[...])


def _rwkv_prep(x, x_prev, p, tm):
    m, d = x.shape
    tm = _row_tile(m, tm)
    row = pl.BlockSpec((tm, d), lambda i: (i, 0))
    full = lambda a: pl.BlockSpec(a.shape, lambda i: (0,) * a.ndim)
    ops = (p["mu"], p["w_r"], p["w_k"], p["w_v"], p["w0"], p["decay_A"], p["decay_B"], p["a0"],
           p["iclr_A"], p["iclr_B"], p["gate_A"], p["gate_B"], p["k_k"], p["k_a"])
    return pl.pallas_call(
        _rwkv_prep_kernel,
        grid=(m // tm,),
        in_specs=[row, row] + [full(a) for a in ops],
        out_specs=[row] * 7,
        out_shape=tuple(jax.ShapeDtypeStruct((m, d), F32) for _ in range(7)),
        name="rwkv_prep",
        compiler_params=_cparams("arbitrary"),
    )(x, x_prev, *ops)


def _mm3(a, b):
    ah, al = _split2(a)
    bh, bl = _split2(b)
    return _dot(ah, bh) + _dot(ah, bl) + _dot(al, bh)


def _mm3_nt(a, b):
    ah, al = _split2(a)
    bh, bl = _split2(b)
    return _dot_nt(ah, bh) + _dot_nt(ah, bl) + _dot_nt(al, bh)


def _rwkv_chunk_kernel(r_ref, lw_ref, k_ref, v_ref, kk_ref, b_ref, s0_ref, rk_ref, gng_ref, gnb_ref,
                       z_ref, sfin_ref, wr_sc, uy_sc, bk_sc, pl_sc, *, L):
    t_len = r_ref.shape[1]
    nchunks = t_len // L
    lane = _iota((1, PAIR), 1)
    head0 = lane < HEAD_DIM
    ri = _iota((L, L), 0)
    ci = _iota((L, L), 1)
    strict = ci < ri
    incl = ci <= ri
    eye = jnp.where(ci == ri, 1.0, 0.0)
    tri = jnp.where(incl, 1.0, 0.0).astype(BF16)
    e2 = _pair_ones()
    nsq = int(math.log2(L)) - 1

    def phase1(c, carry):
        t0 = pl.multiple_of(c * L, L)
        rows = pl.ds(t0, L)
        lw = lw_ref[0, rows, :]
        r = r_ref[0, rows, :]
        k = k_ref[0, rows, :]
        v = v_ref[0, rows, :]
        kk = kk_ref[0, rows, :]
        b = b_ref[0, rows, :]
        hi, mid, lo = _split3(lw)
        cum = _dot(tri, hi) + _dot(tri, mid) + _dot(tri, lo)
        tot = cum[L - 1:L, :]
        at = kk * jnp.exp(cum - lw)
        rt = r * jnp.exp(cum)
        pinv = jnp.exp(-cum)
        bt = b * pinv
        kt = k * pinv
        ptail = jnp.exp(tot - cum)
        w_parts, u_parts, rh_parts, yh_parts = [], [], [], []
        for j in range(2):
            hm = (lane >= HEAD_DIM) if j else head0
            aj = jnp.where(hm, at, 0.0)
            rj = jnp.where(hm, rt, 0.0)
            aab = jnp.where(strict, _mm3_nt(aj, bt), 0.0)
            aak = jnp.where(strict, _mm3_nt(aj, kt), 0.0)
            rb = jnp.where(incl, _mm3_nt(rj, bt), 0.0)
            rkm = jnp.where(incl, _mm3_nt(rj, kt), 0.0)
            pw = -aab
            tinv = eye + pw
            for _ in range(nsq):
                pw = _mm3(pw, pw)
                tinv = tinv + _mm3(tinv, pw)
            wj = -_mm3(tinv, at)
            uj = -_mm3(tinv, _mm3(aak, v))
            w_parts.append(wj)
            u_parts.append(uj)
            rh_parts.append(rt + _mm3(rb, wj))
            yh_parts.append(_mm3(rb, uj) + _mm3(rkm, v))
        pick = lambda parts: jnp.where(head0, parts[0], parts[1])
        wr_sc[c] = jnp.concatenate([pick(w_parts), pick(rh_parts)], axis=0)
        uy_sc[c] = jnp.concatenate([pick(u_parts), pick(yh_parts)], axis=0)
        bk_sc[c] = jnp.concatenate([b * ptail, k * ptail], axis=0)
        pl_sc[c] = jnp.broadcast_to(jnp.exp(tot), (8, PAIR))
        return carry

    lax.fori_loop(0, nchunks, phase1, 0)

    blockdiag = jnp.right_shift(_iota((PAIR, PAIR), 0), 6) == jnp.right_shift(_iota((PAIR, PAIR), 1), 6)
    pad_rows = PAIR - 2 * L

    def phase2(c, s_state):
        t0 = pl.multiple_of(c * L, L)
        rows = pl.ds(t0, L)
        v = v_ref[0, rows, :]
        uy = _mm3_nt(wr_sc[c], s_state) + uy_sc[c]
        u = uy[0:L, :]
        y = uy[L:2 * L, :]
        ym = _head_sum(y, e2) * (1.0 / HEAD_DIM)
        yc = y - ym
        yv = _head_sum(yc * yc, e2) * (1.0 / HEAD_DIM)
        yn = yc * lax.rsqrt(yv + GN_EPS) * gng_ref[...] + gnb_ref[...]
        bonus = _head_sum(r_ref[0, rows, :] * k_ref[0, rows, :] * rk_ref[...], e2) * v
        z_ref[0, rows, :] = yn + bonus
        uv = jnp.concatenate([u, v], axis=0)
        bk = bk_sc[c]
        if pad_rows:
            zpad = jnp.zeros((pad_rows, PAIR), F32)
            uv = jnp.concatenate([uv, zpad], axis=0)
            bk = jnp.concatenate([bk, zpad], axis=0)
        upd = _mm3(uv.T, bk)
        return s_state * pl_sc[c][0:1, :] + jnp.where(blockdiag, upd, 0.0)

    s_fin = lax.fori_loop(0, nchunks, phase2, s0_ref[0, 0])
    sfin_ref[0, 0] = s_fin


def _rwkv_recurrence(r, lw, k, v, kk, b, s0, r_k, gn_g, gn_b, L):
    bsz, t_len, d = r.shape
    hp = d // PAIR
    assert t_len % L == 0
    nchunks = t_len // L
    s0p = s0.reshape(bsz, hp, 2, HEAD_DIM, HEAD_DIM)
    zeros = jnp.zeros_like(s0p[:, :, 0])
    s0bd = jnp.concatenate([jnp.concatenate([s0p[:, :, 0], zeros], axis=-1),
                            jnp.concatenate([zeros, s0p[:, :, 1]], axis=-1)], axis=-2)
    seq = pl.BlockSpec((1, t_len, PAIR), lambda i, j: (i, 0, j))
    vec = pl.BlockSpec((1, PAIR), lambda i, j: (0, j))
    st = pl.BlockSpec((1, 1, PAIR, PAIR), lambda i, j: (i, j, 0, 0))
    z, sfin = pl.pallas_call(
        functools.partial(_rwkv_chunk_kernel, L=L),
        grid=(bsz, hp),
        in_specs=[seq] * 6 + [st, vec, vec, vec],
        out_specs=[seq, st],
        out_shape=(jax.ShapeDtypeStruct((bsz, t_len, d), F32),
                   jax.ShapeDtypeStruct((bsz, hp, PAIR, PAIR), F32)),
        name="rwkv_chunk",
        scratch_shapes=[
            pltpu.VMEM((nchunks, 2 * L, PAIR), F32),
            pltpu.VMEM((nchunks, 2 * L, PAIR), F32),
            pltpu.VMEM((nchunks, 2 * L, PAIR), F32),
            pltpu.VMEM((nchunks, 8, PAIR), F32),
        ],
        compiler_params=_cparams("arbitrary", "arbitrary"),
    )(r, lw, k, v, kk, b, s0bd, r_k, gn_g, gn_b)
    s_heads = jnp.stack([sfin[:, :, :HEAD_DIM, :HEAD_DIM], sfin[:, :, HEAD_DIM:, HEAD_DIM:]], axis=2)
    return z, s_heads.reshape(bsz, 2 * hp, HEAD_DIM, HEAD_DIM)


def _pad_cols(a, n):
    return jnp.pad(a, ((0, 0), (0, n - a.shape[1])))


def _pad_rows(a, n):
    return jnp.pad(a, ((0, n - a.shape[0]), (0, 0)))


def _lora_width(n):
    return -(-n // LANES) * LANES


def kernel(x_prompt, x_sample, cache_k, cache_v, cache_logf, page_table, state_wkv, state_shift, state_conv, fox_w_in, fox_b_f, fox_g_q, fox_g_k, fox_w_out, rwkv_mu, rwkv_w_rkv, rwkv_w0, rwkv_decay_A, rwkv_decay_B, rwkv_a0, rwkv_iclr_A, rwkv_iclr_B, rwkv_gate_A, rwkv_gate_B, rwkv_k_k, rwkv_k_a, rwkv_r_k, rwkv_gn_g, rwkv_gn_b, rwkv_w_out, ffn_w_in, ffn_conv_w, ffn_conv_b, ffn_w_out, ln1_g, ln1_b, ln2_g, ln2_b):
    bp, seq, d = x_prompt.shape
    bd = x_sample.shape[0]
    assert x_sample.shape[1] == 1 and d % PAIR == 0
    heads = d // HEAD_DIM
    f = ffn_w_out.shape[1]
    row = lambda a: a.reshape(1, -1).astype(F32)

    fox_w = fox_w_in[:, :4 * d].astype(BF16)
    fox_wf = _pad_cols(fox_w_in[:, 4 * d:], LANES).astype(BF16)
    fox_bf = _pad_cols(row(fox_b_f), LANES)
    g_q = row(jnp.tile(fox_g_q, heads))
    g_k = row(jnp.tile(fox_g_k, heads))
    fox_wo = fox_w_out.astype(BF16)
    lw_d, lw_i, lw_g = (_lora_width(a.shape[1]) for a in (rwkv_decay_A, rwkv_iclr_A, rwkv_gate_A))
    rw = dict(
        mu=_pad_rows(rwkv_mu, 8), w_r=rwkv_w_rkv[0].astype(BF16), w_k=rwkv_w_rkv[1].astype(BF16),
        w_v=rwkv_w_rkv[2].astype(BF16), w0=row(rwkv_w0),
        decay_A=_pad_cols(rwkv_decay_A, lw_d).astype(BF16), decay_B=_pad_rows(rwkv_decay_B, lw_d).astype(BF16),
        a0=row(rwkv_a0),
        iclr_A=_pad_cols(rwkv_iclr_A, lw_i).astype(BF16), iclr_B=_pad_rows(rwkv_iclr_B, lw_i).astype(BF16),
        gate_A=_pad_cols(rwkv_gate_A, lw_g).astype(BF16), gate_B=_pad_rows(rwkv_gate_B, lw_g).astype(BF16),
        k_k=row(rwkv_k_k), k_a=row(rwkv_k_a))
    r_k, gn_g, gn_b = row(rwkv_r_k), row(rwkv_gn_g), row(rwkv_gn_b)
    rwkv_wo = rwkv_w_out.astype(BF16)
    ffn_wi = ffn_w_in.astype(BF16)
    ffn_wo = ffn_w_out.astype(BF16)

    xp = x_prompt.reshape(bp * seq, d)
    xs = x_sample.reshape(bd, d)
    conv_p, conv_s = [], []

    q, k_p, v_p, kb, vb, gate, lf_p = _fox_proj(xp, fox_w, fox_wf, fox_bf, g_q, g_k, 256)
    c = _cumsum_seq(lf_p.reshape(bp, seq, LANES))
    o = _fox_attention(q.reshape(bp, seq, d), kb.reshape(bp, seq, d), vb.reshape(bp, seq, d), c, 256)
    xp = _gated_out_ln(o.reshape(bp * seq, d), gate, xp, fox_wo, row(ln1_g[0]), row(ln1_b[0]), 512)

    qs, k_s, v_s, _, _, gate_s, lf_s = _fox_proj(xs, fox_w, fox_wf, fox_bf, g_q, g_k, bd)
    os_ = _fox_decode(qs, k_s, v_s, lf_s, cache_k, cache_v, cache_logf, page_table)
    xs = _gated_out_ln(os_, gate_s, xs, fox_wo, row(ln1_g[0]), row(ln1_b[0]), bd)

    def ffn(i, xp, xs):
        args = (ffn_wi[i], ffn_conv_w[i], row(ffn_conv_b[i]), ffn_wo[i], row(ln2_g[i]), row(ln2_b[i]))
        xp, tail = _ffn_seq(xp, seq, *args, 512)
        xs, gt_s = _ffn_step(xs, state_conv[i][:, 0], state_conv[i][:, 1], *args)
        conv_p.append(tail)
        conv_s.append(jnp.stack([state_conv[i][:, 1], gt_s], axis=1))
        return xp, xs

    xp, xs = ffn(0, xp, xs)

    shift_p = xp.reshape(bp, seq, d)[:, -1]
    shift_s = xs
    xp3 = xp.reshape(bp, seq, d)
    xp_prev = jnp.concatenate([jnp.zeros((bp, 1, d), F32), xp3[:, :-1]], axis=1).reshape(bp * seq, d)
    r, lw, k2, v, kk, b, g = _rwkv_prep(xp, xp_prev, rw, 256)
    sh = lambda a: a.reshape(bp, seq, d)
    z, wkv_p = _rwkv_recurrence(sh(r), sh(lw), sh(k2), sh(v), sh(kk), sh(b),
                                jnp.zeros((bp, heads, HEAD_DIM, HEAD_DIM), F32), r_k, gn_g, gn_b, 64)
    xp = _gated_out_ln(z.reshape(bp * seq, d), g, xp, rwkv_wo, row(ln1_g[1]), row(ln1_b[1]), 512)

    outs = _rwkv_prep(xs, state_shift, rw, bd)
    step_len = 16
    pad_t = lambda a: jnp.pad(a.reshape(bd, 1, d), ((0, 0), (0, step_len - 1), (0, 0)))
    rs, lws, k2s, vs, kks, bs = (pad_t(a) for a in outs[:6])
    zs, wkv_s = _rwkv_recurrence(rs, lws, k2s, vs, kks, bs, state_wkv, r_k, gn_g, gn_b, step_len)
    xs = _gated_out_ln(zs[:, 0], outs[6], xs, rwkv_wo, row(ln1_g[1]), row(ln1_b[1]), bd)

    xp, xs = ffn(1, xp, xs)

    hshape = lambda a, n: a.reshape(n, -1, heads, HEAD_DIM)
    return (xp.reshape(bp, seq, d), xs.reshape(bd, 1, d),
            hshape(k_p, bp), hshape(v_p, bp), lf_p[:, :heads].reshape(bp, seq, heads),
            hshape(k_s, bd), hshape(v_s, bd), lf_s[:, :heads].reshape(bd, 1, heads),
            wkv_p, shift_p, wkv_s, shift_s,
            jnp.stack(conv_p), jnp.stack(conv_s))
```
